```python
import jax, jax.numpy as jnp
from jax import lax
import numpy as np

D_MODEL = 1024
BATCH = 32
SEQ = 256
DEPTH = 1
DEC_BATCH = 8
DEC_SEQ = 4096
PAST_LEN = 256

GRID_W = 64
CHUNK = 128
EPS = 1e-6
SSD_HEADS = 8
SSD_HEADDIM = 64
SSD_WIDTH = SSD_HEADS * SSD_HEADDIM
SSD_GROUPS = 2
SSD_STATE = 128
SSD_CONV = 5
XBC_DIM = SSD_WIDTH + 2 * SSD_GROUPS * SSD_STATE
RET_HEADS = 8
RET_KDIM = 64
RET_VDIM = 64
RET_QK_WIDTH = RET_HEADS * RET_KDIM
RET_V_WIDTH = RET_HEADS * RET_VDIM
ROPE_BASE = 10000.0
MIX_WIDTH = SSD_WIDTH + RET_V_WIDTH
PROJ_WIDTH = SSD_WIDTH + XBC_DIM + 2 * SSD_HEADS + 2 * RET_QK_WIDTH + 2 * RET_V_WIDTH
PEER_HEADS = 8
PEER_NKEYS = 128
PEER_EXPERTS = PEER_NKEYS * PEER_NKEYS
PEER_QDIM = 256
PEER_HALF = PEER_QDIM // 2
PEER_TOPK = 16
PEER_BLOCK = 128

kernel_name = 'hybrid_ssd_retention_peer_diffusion_step'


def rmsnorm(x, w):
    x32 = x.astype(jnp.float32)
    y = x32 * lax.rsqrt(jnp.mean(x32 * x32, axis=-1, keepdims=True) + EPS)
    return (y * w).astype(x.dtype)


def group_norm_heads(y, w, b):
    bsz, t, h, p = y.shape
    y32 = y.astype(jnp.float32)
    mu = jnp.mean(y32, axis=-1, keepdims=True)
    var = jnp.mean(jnp.square(y32 - mu), axis=-1, keepdims=True)
    yn = ((y32 - mu) * lax.rsqrt(var + EPS)).reshape(bsz, t, h * p)
    return (yn * w + b).astype(y.dtype)


def depthwise_conv_centred(x, w, bias):
    pad = SSD_CONV // 2
    y = lax.conv_general_dilated(x, w[:, None, :], window_strides=(1,), padding=[(pad, pad)],
                                 dimension_numbers=('NWC', 'WIO', 'NWC'),
                                 feature_group_count=x.shape[-1])
    return y + bias


def rope_2d(x):
    bsz, t, h, d = x.shape
    rows_n = t // GRID_W
    row = jnp.repeat(jnp.arange(rows_n), GRID_W)
    col = jnp.tile(jnp.arange(GRID_W), rows_n)
    half = d // 2
    nf = half // 2
    inv = ROPE_BASE ** (-jnp.arange(nf, dtype=jnp.float32) / nf)

    def rot(xp, pos):
        ang = pos.astype(jnp.float32)[:, None] * inv[None, :]
        cos = jnp.cos(ang)[None, :, None, :].astype(x.dtype)
        sin = jnp.sin(ang)[None, :, None, :].astype(x.dtype)
        x1, x2 = xp[..., :nf], xp[..., nf:]
        return jnp.concatenate([x1 * cos - x2 * sin, x1 * sin + x2 * cos], axis=-1)

    return jnp.concatenate([rot(x[..., :half], row), rot(x[..., half:], col)], axis=-1)


def chunk_scan(q, k, v, log_a, s0, include_diag):
    b, t, h, n = q.shape
    p = v.shape[-1]
    nc = t // CHUNK
    dt = v.dtype
    qc = q.reshape(b, nc, CHUNK, h, n)
    kc = k.reshape(b, nc, CHUNK, h, n)
    vc = v.reshape(b, nc, CHUNK, h, p)
    cum = jnp.cumsum(log_a.astype(jnp.float32).reshape(b, nc, CHUNK, h), axis=2)
    mask = jnp.tril(jnp.ones((CHUNK, CHUNK), dtype=bool), k=0 if include_diag else -1)
    seg = cum[:, :, :, None, :] - cum[:, :, None, :, :]
    decay = jnp.exp(jnp.where(mask[None, None, :, :, None], seg, -jnp.inf)).astype(dt)
    scores = jnp.einsum('bcihn,bcjhn->bcijh', qc, kc) * decay
    y_intra = jnp.einsum('bcijh,bcjhp->bcihp', scores, vc)
    w_end = jnp.exp(cum[:, :, -1:, :] - cum).astype(dt)
    chunk_states = jnp.einsum('bcjhp,bcjhn->bchpn', vc * w_end[..., None], kc)
    chunk_decay = jnp.exp(cum[:, :, -1, :]).astype(dt)

    def step(state, inp):
        st, dec = inp
        return dec[:, :, None, None] * state + st, state

    final, starts = lax.scan(step, s0.astype(dt),
                             (jnp.moveaxis(chunk_states, 1, 0), jnp.moveaxis(chunk_decay, 1, 0)))
    starts = jnp.moveaxis(starts, 0, 1)
    y_inter = jnp.einsum('bcihn,bchpn->bcihp', qc * jnp.exp(cum).astype(dt)[..., None], starts)
    return (y_intra + y_inter).reshape(b, t, h, p), final


def bidir_scan(q, k, v_f, v_b, la_f, la_b, s0_f, s0_b):
    y_f, s_f = chunk_scan(q, k, v_f, la_f, s0_f, True)
    fl = lambda a: jnp.flip(a, axis=1)
    y_b, s_b = chunk_scan(fl(q), fl(k), fl(v_b), fl(la_b), s0_b, False)
    return y_f + fl(y_b), s_f, s_b


def token_mixers(h, s0, latent, w_in, conv_w, conv_b, dt_bias_f, dt_bias_b, a_log_f, a_log_b,
                 d_skip, ssm_norm_w, ret_logit_f, ret_logit_b, ret_gn_w, ret_gn_b, w_out):
    b, t, _ = h.shape
    proj = h @ w_in
    bounds = np.cumsum([SSD_WIDTH, XBC_DIM, 2 * SSD_HEADS, RET_QK_WIDTH, RET_QK_WIDTH, RET_V_WIDTH]).tolist()
    z, xbc, dt_raw, rq, rk, rv, rg = jnp.split(proj, bounds, axis=-1)

    xbc = jax.nn.silu(depthwise_conv_centred(xbc, conv_w, conv_b))
    xs, bm, cm = jnp.split(xbc, [SSD_WIDTH, SSD_WIDTH + SSD_GROUPS * SSD_STATE], axis=-1)
    xs = xs.reshape(b, t, SSD_HEADS, SSD_HEADDIM)
    rep = SSD_HEADS // SSD_GROUPS
    bm = jnp.repeat(bm.reshape(b, t, SSD_GROUPS, SSD_STATE), rep, axis=2)
    cm = jnp.repeat(cm.reshape(b, t, SSD_GROUPS, SSD_STATE), rep, axis=2)
    dt_f = jax.nn.softplus(dt_raw[..., :SSD_HEADS] + dt_bias_f)
    dt_b = jax.nn.softplus(dt_raw[..., SSD_HEADS:] + dt_bias_b)
    y_ssd, ssd_f, ssd_b = bidir_scan(cm, bm, xs * dt_f[..., None], xs * dt_b[..., None],
                                     -dt_f * jnp.exp(a_log_f), -dt_b * jnp.exp(a_log_b), s0[0], s0[1])
    y_ssd = (y_ssd + d_skip[:, None] * xs).reshape(b, t, SSD_WIDTH)
    y_ssd = rmsnorm(y_ssd * jax.nn.silu(z), ssm_norm_w)

    q = rq.reshape(b, t, RET_HEADS, RET_KDIM)
    k = rk.reshape(b, t, RET_HEADS, RET_KDIM) * (RET_KDIM ** -0.5)
    v = rv.reshape(b, t, RET_HEADS, RET_VDIM)
    if latent:
        q = rope_2d(q)
        k = rope_2d(k)
    la_f = jnp.broadcast_to(jax.nn.log_sigmoid(ret_logit_f), (b, t, RET_HEADS))
    la_b = jnp.broadcast_to(jax.nn.log_sigmoid(ret_logit_b), (b, t, RET_HEADS))
    y_ret, ret_f, ret_b = bidir_scan(q, k, v, v, la_f, la_b, s0[2], s0[3])
    y_ret = group_norm_heads(y_ret, ret_gn_w, ret_gn_b) * jax.nn.silu(rg)

    out = jnp.concatenate([y_ssd, y_ret], axis=-1) @ w_out
    return out, (ssd_f, ssd_b, ret_f, ret_b)


def peer(h, wq, keys, u_tab, v_tab):
    b, t, d = h.shape
    q = (h @ wq).reshape(b, t, PEER_HEADS, 2, PEER_HALF)
    s = jnp.einsum('bthsd,hskd->bthsk', q, keys).astype(jnp.float32)
    s1, i1 = lax.top_k(s[..., 0, :], PEER_TOPK)
    s2, i2 = lax.top_k(s[..., 1, :], PEER_TOPK)
    cand = (s1[..., :, None] + s2[..., None, :]).reshape(b, t, PEER_HEADS, PEER_TOPK * PEER_TOPK)
    cand_idx = (i1[..., :, None] * PEER_NKEYS + i2[..., None, :]).reshape(b, t, PEER_HEADS, PEER_TOPK * PEER_TOPK)
    top, pos = lax.top_k(cand, PEER_TOPK)
    idx = jnp.take_along_axis(cand_idx, pos, axis=-1)
    g = jax.nn.softmax(top, axis=-1).astype(h.dtype)
    n_sel = PEER_HEADS * PEER_TOPK
    nb = (b * t) // PEER_BLOCK
    hb = h.reshape(nb, PEER_BLOCK, d)
    ib = idx.reshape(nb, PEER_BLOCK, n_sel)
    gb = g.reshape(nb, PEER_BLOCK, n_sel)

    def expert_block(args):
        hx, ix, gx = args
        act = jax.nn.gelu(jnp.einsum('tkd,td->tk', u_tab[ix], hx))
        return jnp.einsum('tk,tkd->td', gx * act, v_tab[ix])

    return lax.map(expert_block, (hb, ib, gb)).reshape(b, t, d)


def trunk_layer(x, mod, s0, latent, ln1_w, ln2_w, mix_w, peer_w):
    sh1, sc1, g1, sh2, sc2, g2 = jnp.split(mod, 6, axis=-1)
    h = rmsnorm(x, ln1_w) * (1 + sc1) + sh1
    mix, states = token_mixers(h, s0, latent, *mix_w)
    x = x + g1 * mix
    h = rmsnorm(x, ln2_w) * (1 + sc2) + sh2
    x = x + g2 * peer(h, *peer_w)
    return x, states


def setup_inputs(seed: int = 0) -> dict:
    key = jax.random.key(seed)
    ks = jax.random.split(key, 32)
    f32 = jnp.float32
    nrm = lambda k, shape, scale: jax.random.normal(k, shape, f32) * scale

    def dt_bias(k):
        dt = jnp.exp(jax.random.uniform(k, (DEPTH, SSD_HEADS), f32, np.log(1e-3), np.log(1e-1)))
        return dt + jnp.log(-jnp.expm1(-dt))

    hh = jnp.arange(RET_HEADS, dtype=f32)
    gamma = 1.0 - 2.0 ** (-5.0 - hh)
    base_logit = jnp.log(gamma) + (5.0 + hh) * np.float32(np.log(2.0))

    return {
        'x_prompt': nrm(ks[0], (BATCH, SEQ, D_MODEL), 1.0),
        'x_sample': nrm(ks[1], (DEC_BATCH, DEC_SEQ, D_MODEL), 1.0),
        'state_ssd_fwd': nrm(ks[2], (DEC_BATCH, DEPTH, SSD_HEADS, SSD_HEADDIM, SSD_STATE), 0.3),
        'state_ssd_bwd': nrm(ks[3], (DEC_BATCH, DEPTH, SSD_HEADS, SSD_HEADDIM, SSD_STATE), 0.3),
        'state_ret_fwd': nrm(ks[4], (DEC_BATCH, DEPTH, RET_HEADS, RET_VDIM, RET_KDIM), 0.3),
        'state_ret_bwd': nrm(ks[5], (DEC_BATCH, DEPTH, RET_HEADS, RET_VDIM, RET_KDIM), 0.3),
        'c': nrm(ks[6], (DEC_BATCH, D_MODEL), 1.0),
        'c_ctx': nrm(ks[7], (D_MODEL,), 1.0),
        'w_mod': nrm(ks[8], (DEPTH, D_MODEL, 6 * D_MODEL), 0.5 * D_MODEL ** -0.5),
        'b_mod': nrm(ks[9], (DEPTH, 6 * D_MODEL), 0.02),
        'ln1_w': 1.0 + nrm(ks[10], (DEPTH, D_MODEL), 0.05),
        'w_in': nrm(ks[11], (DEPTH, D_MODEL, PROJ_WIDTH), D_MODEL ** -0.5),
        'conv_w': nrm(ks[12], (DEPTH, SSD_CONV, XBC_DIM), SSD_CONV ** -0.5),
        'conv_b': nrm(ks[13], (DEPTH, XBC_DIM), 0.02),
        'dt_bias_f': dt_bias(ks[14]),
        'dt_bias_b': dt_bias(ks[15]),
        'a_log_f': jnp.log(jax.random.uniform(ks[16], (DEPTH, SSD_HEADS), f32, 1.0, 16.0)),
        'a_log_b': jnp.log(jax.random.uniform(ks[17], (DEPTH, SSD_HEADS), f32, 1.0, 16.0)),
        'd_skip': 1.0 + nrm(ks[18], (DEPTH, SSD_HEADS), 0.1),
        'ssm_norm_w': 1.0 + nrm(ks[19], (DEPTH, SSD_WIDTH), 0.05),
        'ret_logit_f': base_logit[None, :] + nrm(ks[20], (DEPTH, RET_HEADS), 0.1),
        'ret_logit_b': base_logit[None, :] + nrm(ks[21], (DEPTH, RET_HEADS), 0.1),
        'ret_gn_w': 1.0 + nrm(ks[22], (DEPTH, RET_V_WIDTH), 0.05),
        'ret_gn_b': nrm(ks[23], (DEPTH, RET_V_WIDTH), 0.02),
        'w_out': nrm(ks[24], (DEPTH, MIX_WIDTH, D_MODEL), MIX_WIDTH ** -0.5),
        'ln2_w': 1.0 + nrm(ks[25], (DEPTH, D_MODEL), 0.05),
        'peer_wq': nrm(ks[26], (DEPTH, D_MODEL, PEER_HEADS * PEER_QDIM), D_MODEL ** -0.5),
        'peer_keys': nrm(ks[27], (DEPTH, PEER_HEADS, 2, PEER_NKEYS, PEER_HALF), PEER_HALF ** -0.5),
        'peer_u': nrm(ks[28], (DEPTH, PEER_EXPERTS, D_MODEL), D_MODEL ** -0.5),
        'peer_v': nrm(ks[29], (DEPTH, PEER_EXPERTS, D_MODEL), 1.0),
        'lnf_w': 1.0 + nrm(ks[30], (D_MODEL,), 0.05),
    }


def reference(x_prompt, x_sample, state_ssd_fwd, state_ssd_bwd, state_ret_fwd, state_ret_bwd, c, c_ctx,
              w_mod, b_mod, ln1_w, w_in, conv_w, conv_b, dt_bias_f, dt_bias_b, a_log_f, a_log_b, d_skip,
              ssm_norm_w, ret_logit_f, ret_logit_b, ret_gn_w, ret_gn_b, w_out, ln2_w,
              peer_wq, peer_keys, peer_u, peer_v, lnf_w):
    ctx, lat = x_prompt, x_sample
    new_ssd_f, new_ssd_b, new_ret_f, new_ret_b = [], [], [], []
    for l in range(DEPTH):
        mix_w = (w_in[l], conv_w[l], conv_b[l], dt_bias_f[l], dt_bias_b[l], a_log_f[l], a_log_b[l],
                 d_skip[l], ssm_norm_w[l], ret_logit_f[l], ret_logit_b[l], ret_gn_w[l], ret_gn_b[l], w_out[l])
        peer_w = (peer_wq[l], peer_keys[l], peer_u[l], peer_v[l])
        mod_ctx = (jax.nn.silu(c_ctx) @ w_mod[l] + b_mod[l])[None, None, :]
        mod_lat = (jax.nn.silu(c) @ w_mod[l] + b_mod[l])[:, None, :]
        nb_ctx = ctx.shape[0]
        z_ssd = jnp.zeros((nb_ctx, SSD_HEADS, SSD_HEADDIM, SSD_STATE), ctx.dtype)
        z_ret = jnp.zeros((nb_ctx, RET_HEADS, RET_VDIM, RET_KDIM), ctx.dtype)
        ctx, st = trunk_layer(ctx, mod_ctx, (z_ssd, z_ssd, z_ret, z_ret), False, ln1_w[l], ln2_w[l], mix_w, peer_w)
        new_ssd_f.append(st[0])
        new_ssd_b.append(st[1])
        new_ret_f.append(st[2])
        new_ret_b.append(st[3])
        s0_lat = (state_ssd_fwd[:, l], state_ssd_bwd[:, l], state_ret_fwd[:, l], state_ret_bwd[:, l])
        lat, _ = trunk_layer(lat, mod_lat, s0_lat, True, ln1_w[l], ln2_w[l], mix_w, peer_w)
    y_prompt = rmsnorm(ctx, lnf_w)
    y_sample = rmsnorm(lat, lnf_w)
    return (y_prompt, y_sample, jnp.stack(new_ssd_f, axis=1), jnp.stack(new_ssd_b, axis=1),
            jnp.stack(new_ret_f, axis=1), jnp.stack(new_ret_b, axis=1))
```

```python
import functools

import numpy as np
import jax
import jax.numpy as jnp
from jax import lax
from jax.experimental import pallas as pl
from jax.experimental.pallas import tpu as pltpu

F32 = jnp.float32
BF16 = jnp.bfloat16

D_MODEL = 1024
GRID_W = 64
CHUNK = 128
EPS = 1e-6
SSD_HEADS = 8
SSD_HEADDIM = 64
SSD_WIDTH = 512
SSD_GROUPS = 2
SSD_STATE = 128
SSD_CONV = 5
XBC_DIM = 1024
RET_HEADS = 8
RET_KDIM = 64
RET_VDIM = 64
RET_WIDTH = 512
ROPE_BASE = 10000.0
PEER_HEADS = 8
PEER_NKEYS = 128
PEER_EXPERTS = PEER_NKEYS * PEER_NKEYS
PEER_QDIM = 256
PEER_HALF = 128
PEER_TOPK = 16

LANES = 128
VMEM_LIMIT = 56 * 1024 * 1024

_C_Z, _C_XBC, _C_Q, _C_K, _C_V, _C_G, _C_DT, _C_QR, _C_KR = 0, 512, 1536, 2048, 2560, 3072, 3584, 3712, 4224
_W_CTX = 3712
_W_LAT = 4736


def _cparams(n_axes):
    return pltpu.CompilerParams(dimension_semantics=("arbitrary",) * n_axes,
                                vmem_limit_bytes=VMEM_LIMIT)


def _sigmoid(x):
    return 1.0 / (1.0 + jnp.exp(-x))


def _silu(x):
    return x * _sigmoid(x)


def _softplus(x):
    return jnp.maximum(x, 0.0) + jnp.log1p(jnp.exp(-jnp.abs(x)))


def _gelu_tanh(x):
    c = float(np.sqrt(2.0 / np.pi))
    return x * (0.5 * (1.0 + jnp.tanh(c * (x + 0.044715 * (x * x * x)))))


def _split3(a):
    hi = a.astype(BF16)
    r = a - hi.astype(F32)
    mid = r.astype(BF16)
    lo = (r - mid.astype(F32)).astype(BF16)
    return hi, mid, lo


def _dot_sel_rhs(a, sel):
    hi, mid, lo = _split3(a)
    d = lambda x: jnp.dot(x, sel, preferred_element_type=F32)
    return d(hi) + d(mid) + d(lo)


def _dot_sel_lhs(sel, a):
    hi, mid, lo = _split3(a)
    d = lambda x: jnp.dot(sel, x, preferred_element_type=F32)
    return d(hi) + d(mid) + d(lo)


def _dot_nt(a, b):
    return lax.dot_general(a, b, (((1,), (1,)), ((), ())), preferred_element_type=F32)


def _dot_tn(a, b):
    return lax.dot_general(a, b, (((0,), (0,)), ((), ())), preferred_element_type=F32)


def _lane_iota(shape):
    return lax.broadcasted_iota(jnp.int32, shape, len(shape) - 1)


def _row_iota(shape):
    return lax.broadcasted_iota(jnp.int32, shape, len(shape) - 2)


def _mod_kernel(c_ref, w_ref, b_ref, o_ref):
    a = _silu(c_ref[...]).astype(BF16)
    o_ref[...] = jnp.dot(a, w_ref[...].astype(BF16), preferred_element_type=F32) + b_ref[...]


def _mod_call(cvec, w_mod, b_mod):
    n = w_mod.shape[1]
    bn = 512
    return pl.pallas_call(
        _mod_kernel,
        out_shape=jax.ShapeDtypeStruct((cvec.shape[0], n), F32),
        grid=(n // bn,),
        in_specs=[pl.BlockSpec(cvec.shape, lambda j: (0, 0)),
                  pl.BlockSpec((D_MODEL, bn), lambda j: (0, j)),
                  pl.BlockSpec((1, bn), lambda j: (0, j))],
        out_specs=pl.BlockSpec((cvec.shape[0], bn), lambda j: (0, j)),
        compiler_params=_cparams(1),
        name="mod",
    )(cvec, w_mod, b_mod)


def _tables_kernel(lg_ref, e2_ref, dret_ref, qe_ref, we_ref, dec_ref):
    L = CHUNK
    la = -_softplus(-lg_ref[...])
    la_exp = _dot_sel_rhs(la, e2_ref[...])
    laf = la_exp[0:1, 0:RET_WIDTH]
    lab = la_exp[0:1, RET_WIDTH:]
    pos = _row_iota((L, RET_WIDTH)).astype(F32)
    qe_ref[:, 0:RET_WIDTH] = jnp.exp((pos + 1.0) * laf)
    qe_ref[:, RET_WIDTH:] = jnp.exp((float(L) - pos) * lab)
    we_ref[:, 0:RET_WIDTH] = jnp.exp((float(L - 1) - pos) * laf)
    we_ref[:, RET_WIDTH:] = jnp.exp(pos * lab)
    dec_ref[...] = jnp.exp(float(L) * la_exp)
    ii = _row_iota((L, L))
    jj = _lane_iota((L, L))
    dist = (ii - jj).astype(F32)
    lane = _lane_iota((1, LANES))
    for h in range(RET_HEADS):
        lf = jnp.sum(jnp.where(lane == h, la[0:1, :], 0.0), axis=1, keepdims=True)
        lb = jnp.sum(jnp.where(lane == RET_HEADS + h, la[0:1, :], 0.0), axis=1, keepdims=True)
        dret_ref[h] = jnp.exp(jnp.where(ii >= jj, dist * lf, -dist * lb))


def _tables_call(logits, e2):
    L = CHUNK
    return pl.pallas_call(
        _tables_kernel,
        out_shape=(jax.ShapeDtypeStruct((RET_HEADS, L, L), F32),
                   jax.ShapeDtypeStruct((L, 2 * RET_WIDTH), F32),
                   jax.ShapeDtypeStruct((L, 2 * RET_WIDTH), F32),
                   jax.ShapeDtypeStruct((8, 2 * RET_WIDTH), F32)),
        compiler_params=_cparams(0),
        name="ret_tables",
    )(logits, e2)


def _inproj_kernel(*refs, rope):
    if rope:
        (x_ref, mod_ref, ln_ref, w_ref, cos_ref, sin_ref,
         z_ref, xbc_ref, q_ref, k_ref, v_ref, g_ref, dt_ref) = refs
    else:
        (x_ref, mod_ref, ln_ref, w_ref,
         z_ref, xbc_ref, q_ref, k_ref, v_ref, g_ref, dt_ref) = refs
    x = x_ref[...]
    ms = jnp.mean(x * x, axis=-1, keepdims=True)
    y = x * lax.rsqrt(ms + EPS) * ln_ref[...]
    h = (y * (1.0 + mod_ref[0, 1:2, :]) + mod_ref[0, 0:1, :]).astype(BF16)

    def proj(lo, n):
        return jnp.dot(h, w_ref[:, lo:lo + n], preferred_element_type=F32)

    z_ref[...] = proj(_C_Z, 512).astype(BF16)
    xbc_ref[...] = proj(_C_XBC, 1024).astype(BF16)
    v_ref[...] = proj(_C_V, 512).astype(BF16)
    g_ref[...] = proj(_C_G, 512).astype(BF16)
    dt_ref[...] = proj(_C_DT, LANES)
    q = proj(_C_Q, 512)
    k = proj(_C_K, 512)
    if rope:
        cs = cos_ref[...]
        sn = sin_ref[...]
        cs = jnp.concatenate([cs, cs, cs, cs], axis=1)
        sn = jnp.concatenate([sn, sn, sn, sn], axis=1)
        q = q * cs + proj(_C_QR, 512) * sn
        k = k * cs + proj(_C_KR, 512) * sn
    q_ref[...] = q.astype(BF16)
    k_ref[...] = (k * (RET_KDIM ** -0.5)).astype(BF16)


def _inproj_call(x2d, mod, ln_w, w_cat, seq_len, per_batch_mod, rope_tabs):
    n = x2d.shape[0]
    tm = 512
    rope = rope_tabs is not None
    nw = w_cat.shape[1]
    mod_idx = (lambda i: ((i * tm) // seq_len, 0, 0)) if per_batch_mod else (lambda i: (0, 0, 0))
    in_specs = [pl.BlockSpec((tm, D_MODEL), lambda i: (i, 0)),
                pl.BlockSpec((1, 6, D_MODEL), mod_idx),
                pl.BlockSpec((1, D_MODEL), lambda i: (0, 0)),
                pl.BlockSpec((D_MODEL, nw), lambda i: (0, 0))]
    args = [x2d, mod, ln_w, w_cat]
    if rope:
        tiles = seq_len // tm
        in_specs += [pl.BlockSpec((tm, LANES), lambda i: (i % tiles, 0)),
                     pl.BlockSpec((tm, LANES), lambda i: (i % tiles, 0))]
        args += list(rope_tabs)
    widths = (512, 1024, 512, 512, 512, 512)
    out_shape = tuple(jax.ShapeDtypeStruct((n, w), BF16) for w in widths) + (jax.ShapeDtypeStruct((n, LANES), F32),)
    out_specs = tuple(pl.BlockSpec((tm, w), lambda i: (i, 0)) for w in widths) + (pl.BlockSpec((tm, LANES), lambda i: (i, 0)),)
    return pl.pallas_call(
        functools.partial(_inproj_kernel, rope=rope),
        out_shape=out_shape,
        grid=(n // tm,),
        in_specs=in_specs,
        out_specs=out_specs,
        compiler_params=_cparams(1),
        name="inproj_lat" if rope else "inproj_ctx",
    )(*args)


def _phase_a_kernel(xm_ref, xp_ref, xn_ref, dt_ref, k_ref, v_ref, cw_ref, cb_ref, dtb_ref, alog_ref,
                    tril_ref, triu_ref, e2_ref, we_ref,
                    act_ref, col_ref, row_ref, sf_ref, sb_ref, rf_ref, rb_ref, dec_ref, pad_scr):
    L = CHUNK
    c = pl.program_id(1)
    nc = pl.num_programs(1)
    prev = xp_ref[8:16, :].astype(F32)
    nxt = xn_ref[0:8, :].astype(F32)
    pad_scr[0:8, :] = jnp.where(c > 0, prev, 0.0)
    pad_scr[8:8 + L, :] = xm_ref[...].astype(F32)
    pad_scr[8 + L:16 + L, :] = jnp.where(c < nc - 1, nxt, 0.0)
    acc = cw_ref[0:1, :] * pad_scr[pl.ds(6, L), :]
    for j in range(1, SSD_CONV):
        acc = acc + cw_ref[j:j + 1, :] * pad_scr[pl.ds(6 + j, L), :]
    act = _silu(acc + cb_ref[...]).astype(BF16)
    act_ref[...] = act

    lane = _lane_iota((L, LANES))
    dtv = _softplus(dt_ref[...] + dtb_ref[...])
    la = jnp.where(lane < 16, -dtv * jnp.exp(alog_ref[...]), 0.0)
    cum_f = _dot_sel_lhs(tril_ref[...], la)
    cum_b = _dot_sel_lhs(triu_ref[...], la)
    cum = jnp.where(lane < 8, cum_f, cum_b)
    dt16 = pltpu.roll(jnp.where(lane < 16, dtv, 0.0), 16, axis=1)
    stats = jnp.where(lane < 16, cum, dt16)
    col_ref[...] = stats
    row_ref[...] = stats.T

    tot = jnp.where(lane[0:1] < 8, cum_f[L - 1:L, :], cum_b[0:1, :])
    wv = jnp.where(lane < 16, jnp.exp(tot - cum) * dtv, 0.0)
    e2 = e2_ref[...]
    wexp = _dot_sel_rhs(wv, e2)
    xs = act[:, 0:SSD_WIDTH].astype(F32)
    xf = (xs * wexp[:, 0:SSD_WIDTH]).astype(BF16)
    xb = (xs * wexp[:, SSD_WIDTH:]).astype(BF16)
    for g in range(SSD_GROUPS):
        bg = act[:, SSD_WIDTH + g * SSD_STATE:SSD_WIDTH + (g + 1) * SSD_STATE]
        sf_ref[0, 0, g] = _dot_tn(bg, xf[:, g * 256:(g + 1) * 256])
        sb_ref[0, 0, g] = _dot_tn(bg, xb[:, g * 256:(g + 1) * 256])
    etot = jnp.where(lane[0:8] < 16, jnp.exp(jnp.broadcast_to(tot, (8, LANES))), 0.0)
    dec_ref[0, 0] = _dot_sel_rhs(etot, e2)

    ii = _row_iota((LANES, LANES))
    jj = _lane_iota((LANES, LANES))
    bd = (ii < 64) == (jj < 64)
    for pr in range(RET_HEADS // 2):
        sl = slice(pr * LANES, (pr + 1) * LANES)
        kp = k_ref[:, sl]
        vp = v_ref[:, sl].astype(F32)
        vf = (vp * we_ref[:, sl]).astype(BF16)
        vb = (vp * we_ref[:, RET_WIDTH + pr * LANES:RET_WIDTH + (pr + 1) * LANES]).astype(BF16)
        rf_ref[0, 0, pr] = jnp.where(bd, _dot_tn(kp, vf), 0.0)
        rb_ref[0, 0, pr] = jnp.where(bd, _dot_tn(kp, vb), 0.0)


def _phase_a_call(xbc, dt, k, v, conv_w, conv_b, dtb, alog, tril, triu, e2, we, nb, seq_len):
    L = CHUNK
    nc = seq_len // L
    n = xbc.shape[0]
    hb = 16
    last_hb = n // hb - 1

    def tok(b, c):
        return (b * nc + c, 0)

    def prev_idx(b, c):
        return (jnp.maximum((b * nc + c) * (L // hb) - 1, 0), 0)

    def next_idx(b, c):
        return (jnp.minimum((b * nc + c + 1) * (L // hb), last_hb), 0)

    const = lambda b, c: (0, 0)
    in_specs = [pl.BlockSpec((L, XBC_DIM), tok),
                pl.BlockSpec((hb, XBC_DIM), prev_idx),
                pl.BlockSpec((hb, XBC_DIM), next_idx),
                pl.BlockSpec((L, LANES), tok),
                pl.BlockSpec((L, RET_WIDTH), tok),
                pl.BlockSpec((L, RET_WIDTH), tok),
                pl.BlockSpec((8, XBC_DIM), const),
                pl.BlockSpec((1, XBC_DIM), const),
                pl.BlockSpec((1, LANES), const),
                pl.BlockSpec((1, LANES), const),
                pl.BlockSpec((L, L), const),
                pl.BlockSpec((L, L), const),
                pl.BlockSpec((LANES, 2 * SSD_WIDTH), const),
                pl.BlockSpec((L, 2 * RET_WIDTH), const)]
    st = lambda b, c: (b, c, 0, 0, 0)
    out_shape = (jax.ShapeDtypeStruct((n, XBC_DIM), BF16),
                 jax.ShapeDtypeStruct((n, LANES), F32),
                 jax.ShapeDtypeStruct((n, L), F32),
                 jax.ShapeDtypeStruct((nb, nc, 2, 128, 256), F32),
                 jax.ShapeDtypeStruct((nb, nc, 2, 128, 256), F32),
                 jax.ShapeDtypeStruct((nb, nc, 4, 128, 128), F32),
                 jax.ShapeDtypeStruct((nb, nc, 4, 128, 128), F32),
                 jax.ShapeDtypeStruct((nb, nc, 8, 2 * SSD_WIDTH), F32))
    out_specs = (pl.BlockSpec((L, XBC_DIM), tok),
                 pl.BlockSpec((L, LANES), tok),
                 pl.BlockSpec((LANES, L), tok),
                 pl.BlockSpec((1, 1, 2, 128, 256), st),
                 pl.BlockSpec((1, 1, 2, 128, 256), st),
                 pl.BlockSpec((1, 1, 4, 128, 128), st),
                 pl.BlockSpec((1, 1, 4, 128, 128), st),
                 pl.BlockSpec((1, 1, 8, 2 * SSD_WIDTH), lambda b, c: (b, c, 0, 0)))
    return pl.pallas_call(
        _phase_a_kernel,
        out_shape=out_shape,
        grid=(nb, nc),
        in_specs=in_specs,
        out_specs=out_specs,
        scratch_shapes=[pltpu.VMEM((L + 16, XBC_DIM), F32)],
        compiler_params=_cparams(2),
        name="phase_a",
    )(xbc, xbc, xbc, dt, k, v, conv_w, conv_b, dtb, alog, tril, triu, e2, we)


def _scan_kernel(locf_ref, locb_ref, decf_ref, decb_ref, initf_ref, initb_ref,
                 stf_ref, stb_ref, finf_ref, finb_ref, *, nc, dec_per_chunk):
    def dec_at(ref, c):
        return ref[0, c, 0:1, :] if dec_per_chunk else ref[0:1, :]

    def fwd(c, s):
        stf_ref[0, c, 0] = s.astype(BF16)
        return dec_at(decf_ref, c) * s + locf_ref[0, c, 0]

    finf_ref[0, 0] = lax.fori_loop(0, nc, fwd, initf_ref[0, 0])

    def bwd(i, s):
        c = nc - 1 - i
        stb_ref[0, c, 0] = s.astype(BF16)
        return dec_at(decb_ref, c) * s + locb_ref[0, c, 0]

    finb_ref[0, 0] = lax.fori_loop(0, nc, bwd, initb_ref[0, 0])


def _scan_call(locf, locb, dec, initf, initb, dec_per_chunk, name):
    nb, nc, ng, rows, width = locf.shape
    loc_spec = pl.BlockSpec((1, nc, 1, rows, width), lambda b, g: (b, 0, g, 0, 0))
    init_spec = pl.BlockSpec((1, 1, rows, width), lambda b, g: (b, g, 0, 0))
    if dec_per_chunk:
        decf_spec = pl.BlockSpec((1, nc, 8, width), lambda b, g: (b, 0, 0, g))
        decb_spec = pl.BlockSpec((1, nc, 8, width), lambda b, g: (b, 0, 0, ng + g))
    else:
        decf_spec = pl.BlockSpec((8, width), lambda b, g: (0, g))
        decb_spec = pl.BlockSpec((8, width), lambda b, g: (0, ng + g))
    return pl.pallas_call(
        functools.partial(_scan_kernel, nc=nc, dec_per_chunk=dec_per_chunk),
        out_shape=(jax.ShapeDtypeStruct(locf.shape, BF16), jax.ShapeDtypeStruct(locf.shape, BF16),
                   jax.ShapeDtypeStruct(initf.shape, F32), jax.ShapeDtypeStruct(initf.shape, F32)),
        grid=(nb, ng),
        in_specs=[loc_spec, loc_spec, decf_spec, decb_spec, init_spec, init_spec],
        out_specs=(loc_spec, loc_spec, init_spec, init_spec),
        compiler_params=_cparams(2),
        name=name,
    )(locf, locb, dec, dec, initf, initb)


def _phase_c_kernel(act_ref, col_ref, row_ref, z_ref, q_ref, k_ref, v_ref, g_ref, x_ref, mod_ref,
                    sf_ref, sb_ref, rf_ref, rb_ref,
                    e2_ref, dret_ref, qe_ref, dskip_ref, ssmw_ref, gnw_ref, gnb_ref, wout_ref, ln2_ref,
                    xmid_ref, h2_ref, slab_scr, *, n_chunks):
    L = CHUNK
    ii = _row_iota((L, L))
    jj = _lane_iota((L, L))
    lower = ii >= jj
    lane = _lane_iota((L, LANES))
    lo_half = lane < 64

    def chunk(ci, carry):
        r0 = pl.multiple_of(ci * L, L)
        rows = pl.ds(r0, L)
        act = act_ref[rows, :]
        col = col_ref[rows, :]
        row = row_ref[rows, :]
        ecol = jnp.exp(jnp.where(lane < 16, col, 0.0))
        eexp = _dot_sel_rhs(ecol, e2_ref[...])

        xs_b = act[:, 0:SSD_WIDTH]
        y_pairs = []
        for g in range(SSD_GROUPS):
            bg = act[:, SSD_WIDTH + g * SSD_STATE:SSD_WIDTH + (g + 1) * SSD_STATE]
            cg = act[:, SSD_WIDTH + (SSD_GROUPS + g) * SSD_STATE:SSD_WIDTH + (SSD_GROUPS + g + 1) * SSD_STATE]
            gm = _dot_nt(cg, bg)
            yi = (jnp.dot(cg, sf_ref[ci, g], preferred_element_type=F32) * eexp[:, g * 256:(g + 1) * 256]
                  + jnp.dot(cg, sb_ref[ci, g], preferred_element_type=F32)
                  * eexp[:, SSD_WIDTH + g * 256:SSD_WIDTH + (g + 1) * 256])
            for pr in range(2):
                wms = []
                for hh in range(2):
                    h = g * 4 + pr * 2 + hh
                    arg = jnp.where(lower, col[:, h:h + 1] - row[h:h + 1, :],
                                    col[:, 8 + h:9 + h] - row[8 + h:9 + h, :])
                    dec = jnp.exp(arg) * jnp.where(lower, row[16 + h:17 + h, :], row[24 + h:25 + h, :])
                    wms.append((gm * dec).astype(BF16))
                lhs = jnp.concatenate(wms, axis=1)
                h1 = g * 4 + pr * 2
                xp = xs_b[:, h1 * 64:h1 * 64 + LANES]
                zero = jnp.zeros_like(xp)
                rhs = jnp.concatenate([jnp.where(lo_half, xp, zero), jnp.where(lo_half, zero, xp)], axis=0)
                y_pairs.append(jnp.dot(lhs, rhs, preferred_element_type=F32) + yi[:, pr * LANES:(pr + 1) * LANES])
        y_ssd = jnp.concatenate(y_pairs, axis=1) + dskip_ref[...] * xs_b.astype(F32)
        y_ssd = y_ssd * _silu(z_ref[rows, :].astype(F32))
        ms = jnp.mean(y_ssd * y_ssd, axis=-1, keepdims=True)
        slab_scr[rows, 0:SSD_WIDTH] = (y_ssd * lax.rsqrt(ms + EPS) * ssmw_ref[...]).astype(BF16)

        qe = qe_ref[...]
        for pr in range(RET_HEADS // 2):
            sl = slice(pr * LANES, (pr + 1) * LANES)
            qp = q_ref[rows, sl]
            kp = k_ref[rows, sl]
            vp = v_ref[rows, sl]
            zero = jnp.zeros_like(qp)
            s1 = _dot_nt(jnp.where(lo_half, qp, zero), kp)
            s2 = _dot_nt(jnp.where(lo_half, zero, qp), kp)
            lhs = jnp.concatenate([(s1 * dret_ref[2 * pr]).astype(BF16),
                                   (s2 * dret_ref[2 * pr + 1]).astype(BF16)], axis=1)
            rhs = jnp.concatenate([jnp.where(lo_half, vp, zero), jnp.where(lo_half, zero, vp)], axis=0)
            y = jnp.dot(lhs, rhs, preferred_element_type=F32)
            qf = qp.astype(F32)
            qd = jnp.concatenate([(qf * qe[:, sl]).astype(BF16),
                                  (qf * qe[:, RET_WIDTH + pr * LANES:RET_WIDTH + (pr + 1) * LANES]).astype(BF16)],
                                 axis=1)
            st = jnp.concatenate([rf_ref[ci, pr], rb_ref[ci, pr]], axis=0)
            y = y + jnp.dot(qd, st, preferred_element_type=F32)
            s_lo = jnp.sum(jnp.where(lo_half, y, 0.0), axis=-1, keepdims=True)
            s_hi = jnp.sum(jnp.where(lo_half, 0.0, y), axis=-1, keepdims=True)
            mu = jnp.where(lo_half, s_lo, s_hi) * (1.0 / RET_VDIM)
            d = y - mu
            v_lo = jnp.sum(jnp.where(lo_half, d * d, 0.0), axis=-1, keepdims=True)
            v_hi = jnp.sum(jnp.where(lo_half, 0.0, d * d), axis=-1, keepdims=True)
            var = jnp.where(lo_half, v_lo, v_hi) * (1.0 / RET_VDIM)
            yn = d * lax.rsqrt(var + EPS) * gnw_ref[:, sl] + gnb_ref[:, sl]
            yn = yn * _silu(g_ref[rows, sl].astype(F32))
            slab_scr[rows, SSD_WIDTH + pr * LANES:SSD_WIDTH + (pr + 1) * LANES] = yn.astype(BF16)
        return carry

    lax.fori_loop(0, n_chunks, chunk, 0)

    mix = jnp.dot(slab_scr[...], wout_ref[...], preferred_element_type=F32)
    xm = x_ref[...] + mod_ref[0, 2:3, :] * mix
    xmid_ref[...] = xm
    ms = jnp.mean(xm * xm, axis=-1, keepdims=True)
    y = xm * lax.rsqrt(ms + EPS) * ln2_ref[...]
    h2_ref[...] = (y * (1.0 + mod_ref[0, 4:5, :]) + mod_ref[0, 3:4, :]).astype(BF16)


def _phase_c_call(act, col, row, z, q, k, v, g, x2d, mod, sf, sb, rf, rb,
                  e2, dret, qe, dskip, ssmw, gnw, gnb, wout, ln2, seq_len, per_batch_mod, n_chunks):
    L = CHUNK
    n = x2d.shape[0]
    tb = n_chunks * L
    nc_total = n // L
    sf = sf.reshape(nc_total, 2, 128, 256)
    sb = sb.reshape(nc_total, 2, 128, 256)
    rf = rf.reshape(nc_total, 4, 128, 128)
    rb = rb.reshape(nc_total, 4, 128, 128)
    tok = lambda i: (i, 0)
    const2 = lambda i: (0, 0)
    mod_idx = (lambda i: ((i * tb) // seq_len, 0, 0)) if per_batch_mod else (lambda i: (0, 0, 0))
    in_specs = [pl.BlockSpec((tb, XBC_DIM), tok),
                pl.BlockSpec((tb, LANES), tok),
                pl.BlockSpec((tb, L), tok),
                pl.BlockSpec((tb, 512), tok), pl.BlockSpec((tb, 512), tok), pl.BlockSpec((tb, 512), tok),
                pl.BlockSpec((tb, 512), tok), pl.BlockSpec((tb, 512), tok),
                pl.BlockSpec((tb, D_MODEL), tok),
                pl.BlockSpec((1, 6, D_MODEL), mod_idx),
                pl.BlockSpec((n_chunks, 2, 128, 256), lambda i: (i, 0, 0, 0)),
                pl.BlockSpec((n_chunks, 2, 128, 256), lambda i: (i, 0, 0, 0)),
                pl.BlockSpec((n_chunks, 4, 128, 128), lambda i: (i, 0, 0, 0)),
                pl.BlockSpec((n_chunks, 4, 128, 128), lambda i: (i, 0, 0, 0)),
                pl.BlockSpec((LANES, 2 * SSD_WIDTH), const2),
                pl.BlockSpec((RET_HEADS, L, L), lambda i: (0, 0, 0)),
                pl.BlockSpec((L, 2 * RET_WIDTH), const2),
                pl.BlockSpec((1, SSD_WIDTH), const2),
                pl.BlockSpec((1, SSD_WIDTH), const2),
                pl.BlockSpec((1, RET_WIDTH), const2),
                pl.BlockSpec((1, RET_WIDTH), const2),
                pl.BlockSpec((D_MODEL, D_MODEL), const2),
                pl.BlockSpec((1, D_MODEL), const2)]
    return pl.pallas_call(
        functools.partial(_phase_c_kernel, n_chunks=n_chunks),
        out_shape=(jax.ShapeDtypeStruct((n, D_MODEL), F32), jax.ShapeDtypeStruct((n, D_MODEL), BF16)),
        grid=(n // tb,),
        in_specs=in_specs,
        out_specs=(pl.BlockSpec((tb, D_MODEL), tok), pl.BlockSpec((tb, D_MODEL), tok)),
        scratch_shapes=[pltpu.VMEM((tb, D_MODEL), BF16)],
        compiler_params=_cparams(1),
        name="phase_c",
    )(act, col, row, z, q, k, v, g, x2d, mod, sf, sb, rf, rb, e2, dret, qe, dskip, ssmw, gnw, gnb, wout, ln2)


def _peer_kernel(h2_ref, xmid_ref, mod_ref, wq_ref, keys_ref, u_ref, vt_ref, lnf_ref, y_ref,
                 qt_scr, s_scr, q_scr, r_scr, p_scr, n_scr, b_scr, x_scr, acc_scr, *, tm, eb):
    e = pl.program_id(1)
    ne = pl.num_programs(1)
    lgs = tm // LANES
    K = PEER_TOPK
    ninf = float("-inf")

    @pl.when(e == 0)
    def _select():
        acc_scr[...] = jnp.zeros_like(acc_scr)
        qt_scr[...] = _dot_nt(wq_ref[...], h2_ref[...]).astype(BF16)
        for hs in range(2 * PEER_HEADS):
            s = jnp.dot(keys_ref[hs], qt_scr[hs * PEER_HALF:(hs + 1) * PEER_HALF, :],
                        preferred_element_type=F32)
            for lg in range(lgs):
                s_scr[hs, lg] = s[:, lg * LANES:(lg + 1) * LANES]

        def top_rows(s):
            rows = []
            cur = s
            for _ in range(K):
                m = jnp.max(cur, axis=0, keepdims=True)
                rows.append(m)
                cur = jnp.where(cur == m, ninf, cur)
            return rows

        def per_head_group(u, carry):
            h = u // lgs
            lg = u % lgs
            s1 = s_scr[2 * h, lg]
            s2 = s_scr[2 * h + 1, lg]
            a = top_rows(s1)
            b = top_rows(s2)
            for r in range(K):
                b_scr[r:r + 1, :] = b[r]
            bm = b_scr[...]
            cands = [a[r] + bm for r in range(K)]
            tops = []
            for _ in range(K):
                m = cands[0]
                for cnd in cands[1:]:
                    m = jnp.maximum(m, cnd)
                m = jnp.max(m, axis=0, keepdims=True)
                tops.append(m)
                cands = [jnp.where(cnd == m, ninf, cnd) for cnd in cands]
            tau = tops[K - 1]
            zsum = jnp.ones_like(tau)
            for r in range(1, K):
                zsum = zsum + jnp.exp(tops[r] - tops[0])
            rank = jnp.zeros_like(s2)
            cnt = jnp.zeros_like(s1)
            for r in range(K):
                rank = rank + jnp.where(b[r] > s2, 1.0, 0.0)
                cnt = cnt + jnp.where((s1 + b[r]) >= tau, 1.0, 0.0)
            r_scr[h, lg] = rank
            q_scr[h, lg] = jnp.exp(s2 - b[0])
            n_scr[h, lg] = jnp.where(s1 >= a[K - 1], cnt, 0.0)
            p_scr[h, lg] = jnp.exp(s1 - a[0]) / zsum
            return carry

        lax.fori_loop(0, PEER_HEADS * lgs, per_head_group, 0)

    at = _dot_nt(u_ref[...], h2_ref[...])
    for j in range(eb // PEER_NKEYS):
        i1 = e * (eb // PEER_NKEYS) + j
        for lg in range(lgs):
            a = at[j * PEER_NKEYS:(j + 1) * PEER_NKEYS, lg * LANES:(lg + 1) * LANES]
            w = jnp.zeros_like(a)
            for h in range(PEER_HEADS):
                nrow = n_scr[h, lg, pl.ds(i1, 1), :]
                prow = p_scr[h, lg, pl.ds(i1, 1), :]
                w = w + jnp.where(r_scr[h, lg] < nrow, q_scr[h, lg], 0.0) * prow
            x_scr[j * PEER_NKEYS:(j + 1) * PEER_NKEYS, lg * LANES:(lg + 1) * LANES] = (w * _gelu_tanh(a)).astype(BF16)
    acc_scr[...] += jnp.dot(vt_ref[...], x_scr[...], preferred_element_type=F32)

    @pl.when(e == ne - 1)
    def _finish():
        xo = xmid_ref[...] + mod_ref[0, 5:6, :] * acc_scr[...].T
        ms = jnp.mean(xo * xo, axis=-1, keepdims=True)
        y_ref[...] = xo * lax.rsqrt(ms + EPS) * lnf_ref[...]


def _peer_call(h2, xmid, mod, wq_t, keys, u_b, vt_b, lnf, seq_len, per_batch_mod, tm, eb):
    n = h2.shape[0]
    lgs = tm // LANES
    mod_idx = (lambda i, e: ((i * tm) // seq_len, 0, 0)) if per_batch_mod else (lambda i, e: (0, 0, 0))
    in_specs = [pl.BlockSpec((tm, D_MODEL), lambda i, e: (i, 0)),
                pl.BlockSpec((tm, D_MODEL), lambda i, e: (i, 0)),
                pl.BlockSpec((1, 6, D_MODEL), mod_idx),
                pl.BlockSpec((PEER_HEADS * PEER_QDIM, D_MODEL), lambda i, e: (0, 0)),
                pl.BlockSpec((2 * PEER_HEADS, PEER_NKEYS, PEER_HALF), lambda i, e: (0, 0, 0)),
                pl.BlockSpec((eb, D_MODEL), lambda i, e: (e, 0)),
                pl.BlockSpec((D_MODEL, eb), lambda i, e: (0, e)),
                pl.BlockSpec((1, D_MODEL), lambda i, e: (0, 0))]
    tab = pltpu.VMEM((PEER_HEADS, lgs, PEER_NKEYS, LANES), F32)
    return pl.pallas_call(
        functools.partial(_peer_kernel, tm=tm, eb=eb),
        out_shape=jax.ShapeDtypeStruct((n, D_MODEL), F32),
        grid=(n // tm, PEER_EXPERTS // eb),
        in_specs=in_specs,
        out_specs=pl.BlockSpec((tm, D_MODEL), lambda i, e: (i, 0)),
        scratch_shapes=[pltpu.VMEM((PEER_HEADS * PEER_QDIM, tm), BF16),
                        pltpu.VMEM((2 * PEER_HEADS, lgs, PEER_NKEYS, LANES), F32),
                        tab, tab, tab, tab,
                        pltpu.VMEM((PEER_TOPK, LANES), F32),
                        pltpu.VMEM((eb, tm), BF16),
                        pltpu.VMEM((D_MODEL, tm), F32)],
        compiler_params=_cparams(2),
        name="peer",
    )(h2, xmid, mod, wq_t, keys, u_b, vt_b, lnf)


def _rope_tables(seq_len):
    nf = RET_KDIM // 4
    inv = ROPE_BASE ** (-jnp.arange(nf, dtype=F32) / nf)
    t = jnp.arange(seq_len)
    row = (t // GRID_W).astype(F32)
    col = (t % GRID_W).astype(F32)
    ang_r = row[:, None] * inv[None, :]
    ang_c = col[:, None] * inv[None, :]
    ang = jnp.concatenate([ang_r, ang_r, ang_c, ang_c], axis=1)
    ang = jnp.concatenate([ang, ang], axis=1)
    return jnp.cos(ang), jnp.sin(ang)


def _rot_columns(w):
    d = np.arange(RET_WIDTH)
    first = (d % 32) < 16
    idx = np.where(first, d + 16, d - 16)
    sign = np.where(first, -1.0, 1.0).astype(np.float32)
    return w[:, idx] * sign[None, :]


def _expand_matrix():
    e = np.zeros((LANES, 2 * SSD_WIDTH), np.float32)
    for h in range(8):
        e[h, h * 64:(h + 1) * 64] = 1.0
        e[8 + h, SSD_WIDTH + h * 64:SSD_WIDTH + (h + 1) * 64] = 1.0
    return jnp.asarray(e, BF16)


def _pad_lanes(*vecs):
    v = jnp.concatenate([jnp.asarray(a, F32).reshape(-1) for a in vecs])
    return jnp.pad(v, (0, LANES - v.shape[0]))[None, :]


def _ssd_state_to_compact(s):
    b = s.shape[0]
    s = s.reshape(b, SSD_GROUPS, 4, SSD_HEADDIM, SSD_STATE)
    return jnp.transpose(s, (0, 1, 4, 2, 3)).reshape(b, SSD_GROUPS, SSD_STATE, 4 * SSD_HEADDIM)


def _ssd_state_from_compact(s):
    b = s.shape[0]
    s = s.reshape(b, SSD_GROUPS, SSD_STATE, 4, SSD_HEADDIM)
    return jnp.transpose(s, (0, 1, 3, 4, 2)).reshape(b, SSD_HEADS, SSD_HEADDIM, SSD_STATE)


def _ret_state_to_pairs(s):
    b = s.shape[0]
    st = jnp.transpose(s, (0, 1, 3, 2)).reshape(b, 4, 2, RET_KDIM, RET_VDIM)
    eye = jnp.eye(2, dtype=s.dtype)
    return jnp.einsum('bqhnp,hk->bqhnkp', st, eye).reshape(b, 4, 2 * RET_KDIM, 2 * RET_VDIM)


def _ret_state_from_pairs(s):
    b = s.shape[0]
    s = s.reshape(b, 4, 2, RET_KDIM, 2, RET_VDIM)
    d = jnp.stack([s[:, :, 0, :, 0, :], s[:, :, 1, :, 1, :]], axis=2)
    return jnp.transpose(d, (0, 1, 2, 4, 3)).reshape(b, RET_HEADS, RET_VDIM, RET_KDIM)


def _trunk(x, mod, per_batch_mod, rope_tabs, init_states, p):
    nb, seq_len, _ = x.shape
    x2d = x.reshape(nb * seq_len, D_MODEL)
    w_cat = p["w_lat"] if rope_tabs is not None else p["w_ctx"]
    z, xbc, q, k, v, g, dt = _inproj_call(x2d, mod, p["ln1"], w_cat, seq_len, per_batch_mod, rope_tabs)
    act, col, row, sf, sb, rf, rb, dec = _phase_a_call(
        xbc, dt, k, v, p["conv_w"], p["conv_b"], p["dtb"], p["alog"], p["tril"], p["triu"], p["e2"], p["we"],
        nb, seq_len)
    ssd_f0, ssd_b0, ret_f0, ret_b0 = init_states
    sfs, sbs, ssd_ff, ssd_bf = _scan_call(sf, sb, dec, ssd_f0, ssd_b0, True, "scan_ssd")
    rfs, rbs, ret_ff, ret_bf = _scan_call(rf, rb, p["ret_dec"], ret_f0, ret_b0, False, "scan_ret")
    n_chunks = 4 if seq_len % (4 * CHUNK) == 0 else 2
    xmid, h2 = _phase_c_call(act, col, row, z, q, k, v, g, x2d, mod, sfs, sbs, rfs, rbs,
                             p["e2"], p["dret"], p["qe"], p["dskip"], p["ssmw"], p["gnw"], p["gnb"],
                             p["wout"], p["ln2"], seq_len, per_batch_mod, n_chunks)
    y = _peer_call(h2, xmid, mod, p["wq_t"], p["keys"], p["u"], p["vt"], p["lnf"],
                   seq_len, per_batch_mod, tm=512, eb=512)
    return y.reshape(nb, seq_len, D_MODEL), (ssd_ff, ssd_bf, ret_ff, ret_bf)


def kernel(x_prompt, x_sample, state_ssd_fwd, state_ssd_bwd, state_ret_fwd, state_ret_bwd, c, c_ctx, w_mod, b_mod, ln1_w, w_in, conv_w, conv_b, dt_bias_f, dt_bias_b, a_log_f, a_log_b, d_skip, ssm_norm_w, ret_logit_f, ret_logit_b, ret_gn_w, ret_gn_b, w_out, ln2_w, peer_wq, peer_keys, peer_u, peer_v, lnf_w):
    depth = w_in.shape[0]
    assert depth == 1, "single-layer trunk"
    l = 0
    nb_ctx = x_prompt.shape[0]
    nb_lat, lat_len, _ = x_sample.shape

    rows = 16
    cvec = jnp.concatenate([c_ctx[None, :], c, jnp.zeros((rows - 1 - nb_lat, D_MODEL), F32)], axis=0)
    mod = _mod_call(cvec, w_mod[l], b_mod[l][None, :]).reshape(rows, 6, D_MODEL)
    mod_ctx = mod[0:1]
    mod_lat = mod[1:1 + nb_lat]

    w = w_in[l]
    wz, wxbc, wdt = w[:, 0:512], w[:, 512:1536], w[:, 1536:1552]
    wq, wk, wv, wg = w[:, 1552:2064], w[:, 2064:2576], w[:, 2576:3088], w[:, 3088:3600]
    wdt = jnp.pad(wdt, ((0, 0), (0, LANES - wdt.shape[1])))
    w_ctx = jnp.concatenate([wz, wxbc, wq, wk, wv, wg, wdt], axis=1).astype(BF16)
    w_lat = jnp.concatenate([w_ctx, _rot_columns(wq).astype(BF16), _rot_columns(wk).astype(BF16)], axis=1)
    e2 = _expand_matrix()
    tri = np.tril(np.ones((CHUNK, CHUNK), np.float32))
    logits = jnp.broadcast_to(_pad_lanes(ret_logit_f[l], ret_logit_b[l]), (8, LANES))
    dret, qe, we, ret_dec = _tables_call(logits, e2)
    expand64 = lambda a: jnp.repeat(a.astype(F32), 64)[None, :]
    p = dict(
        w_ctx=w_ctx, w_lat=w_lat, ln1=ln1_w[l][None, :],
        conv_w=jnp.pad(conv_w[l], ((0, 8 - SSD_CONV), (0, 0))), conv_b=conv_b[l][None, :],
        dtb=_pad_lanes(dt_bias_f[l], dt_bias_b[l]), alog=_pad_lanes(a_log_f[l], a_log_b[l]),
        tril=jnp.asarray(tri, BF16), triu=jnp.asarray(tri.T, BF16), e2=e2,
        we=we, ret_dec=ret_dec, dret=dret, qe=qe,
        dskip=expand64(d_skip[l]), ssmw=ssm_norm_w[l][None, :],
        gnw=ret_gn_w[l][None, :], gnb=ret_gn_b[l][None, :],
        wout=w_out[l].astype(BF16), ln2=ln2_w[l][None, :],
        wq_t=peer_wq[l].T.astype(BF16),
        keys=peer_keys[l].reshape(2 * PEER_HEADS, PEER_NKEYS, PEER_HALF).astype(BF16),
        u=peer_u[l].astype(BF16), vt=peer_v[l].T.astype(BF16), lnf=lnf_w[None, :],
    )

    zs = jnp.zeros((nb_ctx, SSD_GROUPS, SSD_STATE, 4 * SSD_HEADDIM), F32)
    zr = jnp.zeros((nb_ctx, 4, 2 * RET_KDIM, 2 * RET_VDIM), F32)
    y_prompt, (ssd_ff, ssd_bf, ret_ff, ret_bf) = _trunk(x_prompt, mod_ctx, False, None, (zs, zs, zr, zr), p)

    init_lat = (_ssd_state_to_compact(state_ssd_fwd[:, l]), _ssd_state_to_compact(state_ssd_bwd[:, l]),
                _ret_state_to_pairs(state_ret_fwd[:, l]), _ret_state_to_pairs(state_ret_bwd[:, l]))
    y_sample, _ = _trunk(x_sample, mod_lat, True, _rope_tables(lat_len), init_lat, p)

    return (y_prompt, y_sample,
            _ssd_state_from_compact(ssd_ff)[:, None], _ssd_state_from_compact(ssd_bf)[:, None],
            _ret_state_from_pairs(ret_ff)[:, None], _ret_state_from_pairs(ret_bf)[:, None])
```
